```python
import math
import jax
import jax.numpy as jnp
from jax import lax
import numpy as np

D_MODEL = 2048
BATCH = 2
SEQ = 8192
DEPTH = 2
DEC_BATCH = 1
DEC_SEQ = 16384
PAST_LEN = 128

GRID_W = 64
HEAD_DIM = 128
NA_HEADS = 4
NA_ROWS = 8
NA_COLS = 16
NA_QW = 16
NA_KW = NA_QW + NA_COLS
MLA_HEADS = 4
MLA_Q_LORA = 384
MLA_KV_LORA = 128
MLA_NOPE = 128
MLA_ROPE = 64
MLA_V = 128
GQA_HEADS = 4
GQA_KV_HEADS = 2
SSM_HEADS = 8
SSM_HEAD_DIM = 64
SSM_INNER = SSM_HEADS * SSM_HEAD_DIM
SSM_GROUPS = 2
SSM_STATE = 128
SSM_CONV = 5
SSM_CHUNK = 128
SSM_CONV_CH = SSM_INNER + 2 * SSM_GROUPS * SSM_STATE
DT_MIN = 0.001
DT_MAX = 0.1
ATTN_BLOCK = 128
ROPE_THETA = 10000.0
N_KEYS = 128
N_EXPERTS = N_KEYS * N_KEYS
PEER_HEADS = 8
PEER_TOPK = 16
PEER_DK = 256
PEER_BLOCK = 128
EPS = 1e-6
NA_W = NA_HEADS * HEAD_DIM
MLA_OUT_W = MLA_HEADS * MLA_V
GQA_W = GQA_HEADS * HEAD_DIM
GQA_KV_W = GQA_KV_HEADS * HEAD_DIM
MIX_W = NA_W + MLA_OUT_W + GQA_W + SSM_INNER
IN_SIZES = (NA_W, NA_W, NA_W,
            MLA_Q_LORA, MLA_KV_LORA, MLA_ROPE,
            GQA_W, GQA_KV_W, GQA_KV_W,
            SSM_INNER, SSM_CONV_CH, 2 * SSM_HEADS)
IN_W = sum(IN_SIZES)

kernel_name = 'hybrid_parallel_head_encoder_peer'


def rms_norm(x, g):
    xf = x.astype(jnp.float32)
    y = xf * lax.rsqrt(jnp.mean(xf * xf, axis=-1, keepdims=True) + EPS)
    return (y * g.astype(jnp.float32)).astype(x.dtype)


def split_cols(h, sizes):
    offs = [int(o) for o in np.cumsum(sizes)[:-1]]
    return jnp.split(h, offs, axis=-1)


def rope(x, pos):
    half = x.shape[-1] // 2
    freqs = ROPE_THETA ** (-jnp.arange(half, dtype=jnp.float32) / half)
    ang = pos[:, None] * freqs[None, :]
    cos = jnp.cos(ang)[None, :, None, :]
    sin = jnp.sin(ang)[None, :, None, :]
    xf = x.astype(jnp.float32)
    x1, x2 = xf[..., :half], xf[..., half:]
    return jnp.concatenate([x1 * cos - x2 * sin, x2 * cos + x1 * sin], axis=-1).astype(x.dtype)


def axial_rope(x, row_pos, col_pos):
    half = x.shape[-1] // 2
    return jnp.concatenate([rope(x[..., :half], row_pos), rope(x[..., half:], col_pos)], axis=-1)


def dense_attention(q, k, v):
    b, s, hkv, g, dq = q.shape
    nq = s // ATTN_BLOCK
    scale = dq ** -0.5
    qb = jnp.moveaxis(q.reshape(b, nq, ATTN_BLOCK, hkv, g, dq), 1, 0)

    def one_block(qi):
        sc = jnp.einsum('bqkgd,bskd->bkgqs', qi, k).astype(jnp.float32) * scale
        p = jax.nn.softmax(sc, axis=-1).astype(v.dtype)
        return jnp.einsum('bkgqs,bskd->bqkgd', p, v)

    out = lax.map(one_block, qb)
    return jnp.moveaxis(out, 0, 1).reshape(b, s, hkv * g * v.shape[-1])


def neighbourhood_attention(q, k, v, rpb):
    b, s, nh, d = q.shape
    rows = s // GRID_W
    wr = min(NA_ROWS, rows)
    nb = GRID_W // NA_QW
    r = jnp.arange(rows)
    row_idx = jnp.clip(r - wr // 2, 0, rows - wr)[:, None] + jnp.arange(wr)[None, :]
    blk = jnp.arange(nb)
    col_idx = jnp.clip(blk * NA_QW - NA_COLS // 2, 0, GRID_W - NA_KW)[:, None] + jnp.arange(NA_KW)[None, :]
    key_idx = (row_idx[:, None, :, None] * GRID_W + col_idx[None, :, None, :]).reshape(rows, nb, wr * NA_KW)
    kg = jnp.take(k, key_idx, axis=1)
    vg = jnp.take(v, key_idx, axis=1)
    qg = q.reshape(b, rows, nb, NA_QW, nh, d)
    scores = jnp.einsum('brnqhd,brnkhd->brnhqk', qg, kg).astype(jnp.float32) * (d ** -0.5)
    scores = scores.reshape(b, rows, nb, nh, NA_QW, wr, NA_KW)
    q_col = blk[:, None] * NA_QW + jnp.arange(NA_QW)[None, :]
    win_lo = jnp.clip(q_col - NA_COLS // 2, 0, GRID_W - NA_COLS)[:, :, None]
    k_col = col_idx[:, None, :]
    in_win = (k_col >= win_lo) & (k_col < win_lo + NA_COLS)
    d_row = row_idx - r[:, None] + (NA_ROWS - 1)
    d_col = jnp.clip(k_col - q_col[:, :, None] + (NA_COLS - 1), 0, 2 * NA_COLS - 2)
    bias = rpb[:, d_row[:, None, None, :, None], d_col[None, :, :, None, :]]
    scores = scores + jnp.moveaxis(bias, 0, 2).astype(jnp.float32)[None]
    scores = jnp.where(in_win[None, None, :, None, :, None, :], scores, -jnp.inf)
    probs = jax.nn.softmax(scores.reshape(b, rows, nb, nh, NA_QW, wr * NA_KW), axis=-1).astype(v.dtype)
    out = jnp.einsum('brnhqk,brnkhd->brnqhd', probs, vg)
    return out.reshape(b, s, nh * d)


def depthwise_conv(x, w, bias):
    ch = x.shape[-1]
    y = lax.conv_general_dilated(x, w[:, None, :], window_strides=(1,),
                                 padding=[(SSM_CONV // 2, SSM_CONV // 2)],
                                 dimension_numbers=('NWC', 'WIO', 'NWC'),
                                 feature_group_count=ch)
    return y + bias


def ssd_scan(x, dt, a, bm, cm, d_skip):
    b, s, nh, p = x.shape
    g, n = bm.shape[2], bm.shape[3]
    r = nh // g
    nc = s // SSM_CHUNK
    f32 = jnp.float32
    xs = x.astype(f32).reshape(b, nc, SSM_CHUNK, g, r, p)
    dts = dt.reshape(b, nc, SSM_CHUNK, g, r)
    bs = bm.astype(f32).reshape(b, nc, SSM_CHUNK, g, n)
    cs = cm.astype(f32).reshape(b, nc, SSM_CHUNK, g, n)
    a_cum = jnp.cumsum(jnp.moveaxis(dts * a.reshape(g, r), 2, -1), axis=-1)
    lower = jnp.tril(jnp.ones((SSM_CHUNK, SSM_CHUNK), dtype=bool))
    seg = a_cum[..., :, None] - a_cum[..., None, :]
    decay = jnp.exp(jnp.where(lower, seg, -jnp.inf))
    xdt = xs * dts[..., None]
    cb = jnp.einsum('bcign,bcjgn->bcgij', cs, bs)
    y_diag = jnp.einsum('bcgij,bcgrij,bcjgrp->bcigrp', cb, decay, xdt)
    decay_to_end = jnp.exp(a_cum[..., -1:] - a_cum)
    chunk_states = jnp.einsum('bcjgn,bcgrj,bcjgrp->bcgrpn', bs, decay_to_end, xdt)
    chunk_decay = jnp.exp(a_cum[..., -1])

    def carry_state(h, inp):
        st, dec = inp
        return h * dec[..., None, None] + st, h

    _, prev = lax.scan(carry_state, jnp.zeros((b, g, r, p, n), f32),
                       (jnp.moveaxis(chunk_states, 1, 0), jnp.moveaxis(chunk_decay, 1, 0)))
    prev = jnp.moveaxis(prev, 0, 1)
    y_off = jnp.einsum('bcign,bcgrpn,bcgri->bcigrp', cs, prev, jnp.exp(a_cum))
    y = y_diag + y_off + xs * d_skip.astype(f32).reshape(g, r, 1)
    return y.reshape(b, s, nh, p).astype(x.dtype)


def ssd_mixer(z, xbc, dt_raw, conv_w, conv_b, a_log, dt_bias, d_skip, norm_g):
    b, s, _ = z.shape
    xbc = jax.nn.silu(depthwise_conv(xbc, conv_w, conv_b))
    xs, bm, cm = split_cols(xbc, (SSM_INNER, SSM_GROUPS * SSM_STATE, SSM_GROUPS * SSM_STATE))
    xs = xs.reshape(b, s, SSM_HEADS, SSM_HEAD_DIM)
    bm = bm.reshape(b, s, SSM_GROUPS, SSM_STATE)
    cm = cm.reshape(b, s, SSM_GROUPS, SSM_STATE)
    dt = jax.nn.softplus(dt_raw.astype(jnp.float32).reshape(b, s, 2, SSM_HEADS) + dt_bias.astype(jnp.float32))
    a = -jnp.exp(a_log.astype(jnp.float32))
    y_fwd = ssd_scan(xs, dt[:, :, 0], a[0], bm, cm, d_skip[0])
    y_bwd = jnp.flip(ssd_scan(jnp.flip(xs, 1), jnp.flip(dt[:, :, 1], 1), a[1],
                              jnp.flip(bm, 1), jnp.flip(cm, 1), d_skip[1]), 1)
    y = (y_fwd + y_bwd).reshape(b, s, SSM_INNER)
    return rms_norm(y * jax.nn.silu(z), norm_g)


def peer(x, w_query, sub_keys, u, v):
    b, s, d = x.shape
    half = PEER_DK // 2
    xt = x.reshape((b * s) // PEER_BLOCK, PEER_BLOCK, d)

    def one_block(xb):
        q = (xb @ w_query).reshape(PEER_BLOCK, PEER_HEADS, 2, half)
        sc = jnp.einsum('thpd,hpkd->thpk', q, sub_keys).astype(jnp.float32)
        top_s, top_i = lax.top_k(sc, PEER_TOPK)
        cand = top_s[:, :, 0, :, None] + top_s[:, :, 1, None, :]
        best_s, best_c = lax.top_k(cand.reshape(PEER_BLOCK, PEER_HEADS, PEER_TOPK * PEER_TOPK), PEER_TOPK)
        i1 = jnp.take_along_axis(top_i[:, :, 0, :], best_c // PEER_TOPK, axis=-1)
        i2 = jnp.take_along_axis(top_i[:, :, 1, :], best_c % PEER_TOPK, axis=-1)
        idx = i1 * N_KEYS + i2
        gates = jax.nn.softmax(best_s, axis=-1)
        hid = jnp.einsum('td,thkd->thk', xb, jnp.take(u, idx, axis=0))
        act = (jax.nn.gelu(hid.astype(jnp.float32), approximate=False) * gates).astype(xb.dtype)
        return jnp.einsum('thk,thkd->td', act, jnp.take(v, idx, axis=0))

    return lax.map(one_block, xt).reshape(b, s, d)


def trunk(x, norm1, w_in, na_rpb, mla_q_a_norm, mla_kv_a_norm, mla_w_q_b, mla_w_kv_b,
          gqa_q_norm, gqa_k_norm, ssm_conv_w, ssm_conv_b, ssm_a_log, ssm_dt_bias, ssm_d, ssm_norm,
          out_norm_na, out_norm_mla, out_norm_gqa, w_out, norm2, peer_w_query, peer_sub_keys,
          peer_u, peer_v, final_norm):
    b, s, _ = x.shape
    ar = jnp.arange(s)
    pos = ar.astype(jnp.float32)
    row_pos = (ar // GRID_W).astype(jnp.float32)
    col_pos = (ar % GRID_W).astype(jnp.float32)
    for l in range(DEPTH):
        h = rms_norm(x, norm1[l])
        (na_q, na_k, na_v, mla_qa, mla_kva, mla_kr, g_q, g_k, g_v,
         ssm_z, ssm_xbc, ssm_dt) = split_cols(h @ w_in[l], IN_SIZES)
        o_na = neighbourhood_attention(na_q.reshape(b, s, NA_HEADS, HEAD_DIM),
                                       na_k.reshape(b, s, NA_HEADS, HEAD_DIM),
                                       na_v.reshape(b, s, NA_HEADS, HEAD_DIM), na_rpb[l])
        q = (rms_norm(mla_qa, mla_q_a_norm[l]) @ mla_w_q_b[l]).reshape(b, s, MLA_HEADS, MLA_NOPE + MLA_ROPE)
        kv = (rms_norm(mla_kva, mla_kv_a_norm[l]) @ mla_w_kv_b[l]).reshape(b, s, MLA_HEADS, MLA_NOPE + MLA_V)
        q = jnp.concatenate([q[..., :MLA_NOPE], rope(q[..., MLA_NOPE:], pos)], axis=-1)
        k_rot = jnp.broadcast_to(rope(mla_kr[:, :, None, :], pos), (b, s, MLA_HEADS, MLA_ROPE))
        k = jnp.concatenate([kv[..., :MLA_NOPE], k_rot], axis=-1)
        o_mla = dense_attention(q[:, :, :, None, :], k, kv[..., MLA_NOPE:])
        gq = axial_rope(rms_norm(g_q.reshape(b, s, GQA_HEADS, HEAD_DIM), gqa_q_norm[l]), row_pos, col_pos)
        gk = axial_rope(rms_norm(g_k.reshape(b, s, GQA_KV_HEADS, HEAD_DIM), gqa_k_norm[l]), row_pos, col_pos)
        o_gqa = dense_attention(gq.reshape(b, s, GQA_KV_HEADS, GQA_HEADS // GQA_KV_HEADS, HEAD_DIM),
                                gk, g_v.reshape(b, s, GQA_KV_HEADS, HEAD_DIM))
        o_ssm = ssd_mixer(ssm_z, ssm_xbc, ssm_dt, ssm_conv_w[l], ssm_conv_b[l], ssm_a_log[l],
                          ssm_dt_bias[l], ssm_d[l], ssm_norm[l])
        mixed = jnp.concatenate([rms_norm(o_na, out_norm_na[l]), rms_norm(o_mla, out_norm_mla[l]),
                                 rms_norm(o_gqa, out_norm_gqa[l]), o_ssm], axis=-1)
        x = x + mixed @ w_out[l]
        x = x + peer(rms_norm(x, norm2[l]), peer_w_query[l], peer_sub_keys[l], peer_u[l], peer_v[l])
    return rms_norm(x, final_norm)


def setup_inputs(seed: int = 0) -> dict:
    key = jax.random.key(seed)
    ks = jax.random.split(key, 28)
    f32 = jnp.float32

    def normal(k, shape, scale):
        return jax.random.normal(k, shape, f32) * scale

    def gain(k, shape):
        return 1.0 + 0.01 * jax.random.normal(k, shape, f32)

    L = DEPTH
    dt0 = jnp.exp(jax.random.uniform(ks[14], (L, 2, SSM_HEADS), f32)
                  * (math.log(DT_MAX) - math.log(DT_MIN)) + math.log(DT_MIN))
    return {
        'x_prompt': normal(ks[0], (BATCH, SEQ, D_MODEL), 1.0),
        'x_sample': normal(ks[1], (DEC_BATCH, DEC_SEQ, D_MODEL), 1.0),
        'norm1': gain(ks[2], (L, D_MODEL)),
        'w_in': normal(ks[3], (L, D_MODEL, IN_W), D_MODEL ** -0.5),
        'na_rpb': normal(ks[4], (L, NA_HEADS, 2 * NA_ROWS - 1, 2 * NA_COLS - 1), 0.1),
        'mla_q_a_norm': gain(ks[5], (L, MLA_Q_LORA)),
        'mla_kv_a_norm': gain(ks[6], (L, MLA_KV_LORA)),
        'mla_w_q_b': normal(ks[7], (L, MLA_Q_LORA, MLA_HEADS * (MLA_NOPE + MLA_ROPE)), MLA_Q_LORA ** -0.5),
        'mla_w_kv_b': normal(ks[8], (L, MLA_KV_LORA, MLA_HEADS * (MLA_NOPE + MLA_V)), MLA_KV_LORA ** -0.5),
        'gqa_q_norm': gain(ks[9], (L, HEAD_DIM)),
        'gqa_k_norm': gain(ks[10], (L, HEAD_DIM)),
        'ssm_conv_w': normal(ks[11], (L, SSM_CONV, SSM_CONV_CH), SSM_CONV ** -0.5),
        'ssm_conv_b': normal(ks[12], (L, SSM_CONV_CH), 0.01),
        'ssm_a_log': jnp.log(jax.random.uniform(ks[13], (L, 2, SSM_HEADS), f32, 1.0, 16.0)),
        'ssm_dt_bias': dt0 + jnp.log(-jnp.expm1(-dt0)),
        'ssm_d': 1.0 + normal(ks[15], (L, 2, SSM_HEADS), 0.1),
        'ssm_norm': gain(ks[16], (L, SSM_INNER)),
        'out_norm_na': gain(ks[17], (L, NA_W)),
        'out_norm_mla': gain(ks[18], (L, MLA_OUT_W)),
        'out_norm_gqa': gain(ks[19], (L, GQA_W)),
        'w_out': normal(ks[20], (L, MIX_W, D_MODEL), MIX_W ** -0.5),
        'norm2': gain(ks[21], (L, D_MODEL)),
        'peer_w_query': normal(ks[22], (L, D_MODEL, PEER_HEADS * PEER_DK), D_MODEL ** -0.5),
        'peer_sub_keys': normal(ks[23], (L, PEER_HEADS, 2, N_KEYS, PEER_DK // 2), (PEER_DK // 2) ** -0.5),
        'peer_u': normal(ks[24], (L, N_EXPERTS, D_MODEL), D_MODEL ** -0.5),
        'peer_v': normal(ks[25], (L, N_EXPERTS, D_MODEL), 0.5),
        'final_norm': gain(ks[26], (D_MODEL,)),
    }


def reference(x_prompt, x_sample, norm1, w_in, na_rpb, mla_q_a_norm, mla_kv_a_norm, mla_w_q_b,
              mla_w_kv_b, gqa_q_norm, gqa_k_norm, ssm_conv_w, ssm_conv_b, ssm_a_log, ssm_dt_bias,
              ssm_d, ssm_norm, out_norm_na, out_norm_mla, out_norm_gqa, w_out, norm2, peer_w_query,
              peer_sub_keys, peer_u, peer_v, final_norm):
    params = (norm1, w_in, na_rpb, mla_q_a_norm, mla_kv_a_norm, mla_w_q_b, mla_w_kv_b,
              gqa_q_norm, gqa_k_norm, ssm_conv_w, ssm_conv_b, ssm_a_log, ssm_dt_bias, ssm_d,
              ssm_norm, out_norm_na, out_norm_mla, out_norm_gqa, w_out, norm2, peer_w_query,
              peer_sub_keys, peer_u, peer_v, final_norm)
    y_prompt = trunk(x_prompt, *params)
    y_sample = trunk(x_sample, *params)
    return (y_prompt, y_sample)
```

```python
import functools
import math

import numpy as np
import jax
import jax.numpy as jnp
from jax import lax
from jax.experimental import pallas as pl
from jax.experimental.pallas import tpu as pltpu

F32 = jnp.float32
BF16 = jnp.bfloat16

EPS = 1e-6
NEG = -1e30
LANES = 128

GRID_W = 64
HEAD_DIM = 128
NA_HEADS = 4
NA_ROWS = 8
NA_COLS = 16
MLA_HEADS = 4
MLA_Q_LORA = 384
MLA_KV_LORA = 128
MLA_NOPE = 128
MLA_ROPE = 64
MLA_V = 128
MLA_QPAD = 256
GQA_HEADS = 4
GQA_KV_HEADS = 2
SSM_HEADS = 8
SSM_HEAD_DIM = 64
SSM_INNER = SSM_HEADS * SSM_HEAD_DIM
SSM_GROUPS = 2
SSM_STATE = 128
SSM_CONV = 5
SSM_CHUNK = 128
ROPE_THETA = 10000.0
ROPE_HALF = 32
N_KEYS = 128
PEER_HEADS = 8
PEER_TOPK = 16

NA_W = NA_HEADS * HEAD_DIM
GQA_W = GQA_HEADS * HEAD_DIM
GQA_KV_W = GQA_KV_HEADS * HEAD_DIM
SSM_BC = SSM_GROUPS * SSM_STATE
SSM_CONV_CH = SSM_INNER + 2 * SSM_BC

VMEM_LIMIT = 56 * 1024 * 1024


def _params(*sem):
    return pltpu.CompilerParams(dimension_semantics=sem, vmem_limit_bytes=VMEM_LIMIT)


def _lanes(g, n):
    reps = n // LANES
    return g if reps == 1 else jnp.concatenate([g] * reps, axis=1)


def _rms_cols(x, g):
    ms = jnp.mean(x * x, axis=0, keepdims=True)
    return (x * lax.rsqrt(ms + EPS)) * _lanes(g, x.shape[1])


def _rope_half(x, cos, sin):
    x1, x2 = x[0:ROPE_HALF], x[ROPE_HALF:2 * ROPE_HALF]
    return jnp.concatenate([x1 * cos - x2 * sin, x2 * cos + x1 * sin], axis=0)


def _dot(a, b):
    return jnp.dot(a, b, preferred_element_type=F32)


def _const_spec(shape):
    nd = len(shape)
    return pl.BlockSpec(shape, lambda *_: (0,) * nd)


def _transpose_in_kernel(x_ref, o_ref):
    o_ref[...] = x_ref[...].T


def transpose_in(x2d, tm=512):
    tg, d = x2d.shape
    return pl.pallas_call(
        _transpose_in_kernel, grid=(tg // tm,),
        in_specs=[pl.BlockSpec((tm, d), lambda i: (i, 0))],
        out_specs=pl.BlockSpec((d, tm), lambda i: (0, i)),
        out_shape=jax.ShapeDtypeStruct((d, tg), F32),
        compiler_params=_params("parallel"), name="transpose_in")(x2d)


def _final_norm_kernel(x_ref, g_ref, o_ref):
    o_ref[...] = _rms_cols(x_ref[...], g_ref[...]).T


def final_norm_out(xt, g, tm=512):
    d, tg = xt.shape
    return pl.pallas_call(
        _final_norm_kernel, grid=(tg // tm,),
        in_specs=[pl.BlockSpec((d, tm), lambda i: (0, i)), _const_spec((d, LANES))],
        out_specs=pl.BlockSpec((tm, d), lambda i: (i, 0)),
        out_shape=jax.ShapeDtypeStruct((tg, d), F32),
        compiler_params=_params("parallel"), name="final_norm")(xt, g)


def _proj_a_kernel(x_ref, g1_ref, w_ref, gq_ref, gk_ref, rope_ref,
                   naq_ref, nak_ref, nav_ref, gqo_ref, gko_ref, gvo_ref):
    xn = _rms_cols(x_ref[...], g1_ref[...]).astype(BF16)

    def proj(lo, hi):
        return _dot(w_ref[lo:hi, :], xn)

    o = 0
    naq_ref[...] = (proj(o, o + NA_W) * (HEAD_DIM ** -0.5)).astype(BF16)
    o += NA_W
    nak_ref[...] = proj(o, o + NA_W).T.astype(BF16)
    o += NA_W
    nav_ref[...] = proj(o, o + NA_W).astype(BF16)
    o += NA_W

    cr, sr = rope_ref[2 * ROPE_HALF:3 * ROPE_HALF], rope_ref[3 * ROPE_HALF:4 * ROPE_HALF]
    cc, sc = rope_ref[4 * ROPE_HALF:5 * ROPE_HALF], rope_ref[5 * ROPE_HALF:6 * ROPE_HALF]

    def norm_rope(hh, g):
        y = _rms_cols(hh, g)
        return jnp.concatenate([_rope_half(y[0:64], cr, sr), _rope_half(y[64:128], cc, sc)], axis=0)

    gq = proj(o, o + GQA_W)
    o += GQA_W
    gqo_ref[...] = (jnp.concatenate(
        [norm_rope(gq[h * HEAD_DIM:(h + 1) * HEAD_DIM], gq_ref[...]) for h in range(GQA_HEADS)], axis=0)
        * (HEAD_DIM ** -0.5)).astype(BF16)
    gk = proj(o, o + GQA_KV_W)
    o += GQA_KV_W
    gko_ref[...] = jnp.concatenate(
        [norm_rope(gk[h * HEAD_DIM:(h + 1) * HEAD_DIM], gk_ref[...]) for h in range(GQA_KV_HEADS)],
        axis=0).T.astype(BF16)
    gvo_ref[...] = proj(o, o + GQA_KV_W).astype(BF16)


def proj_a(xt, g1, wa, gq, gk, rope, tm=256):
    d, tg = xt.shape
    col = lambda f, dt: jax.ShapeDtypeStruct((f, tg), dt)
    row = lambda f, dt: jax.ShapeDtypeStruct((tg, f), dt)
    cspec = lambda f: pl.BlockSpec((f, tm), lambda i: (0, i))
    rspec = lambda f: pl.BlockSpec((tm, f), lambda i: (i, 0))
    return pl.pallas_call(
        _proj_a_kernel, grid=(tg // tm,),
        in_specs=[cspec(d), _const_spec((d, LANES)), _const_spec(wa.shape),
                  _const_spec((HEAD_DIM, LANES)), _const_spec((HEAD_DIM, LANES)), cspec(6 * ROPE_HALF)],
        out_specs=[cspec(NA_W), rspec(NA_W), cspec(NA_W), cspec(GQA_W), rspec(GQA_KV_W), cspec(GQA_KV_W)],
        out_shape=[col(NA_W, BF16), row(NA_W, BF16), col(NA_W, BF16),
                   col(GQA_W, BF16), row(GQA_KV_W, BF16), col(GQA_KV_W, BF16)],
        compiler_params=_params("parallel"), name="proj_a")(xt, g1, wa, gq, gk, rope)


def _proj_b_kernel(x_ref, g1_ref, w_ref, gqa_ref, gkva_ref, wqb_ref, wkvb_ref, rope_ref,
                   mq_ref, mk_ref, mv_ref):
    tm = x_ref.shape[1]
    xn = _rms_cols(x_ref[...], g1_ref[...]).astype(BF16)
    h = _dot(w_ref[...], xn)
    qa = h[0:MLA_Q_LORA]
    kva = h[MLA_Q_LORA:MLA_Q_LORA + MLA_KV_LORA]
    kr = h[MLA_Q_LORA + MLA_KV_LORA:MLA_Q_LORA + MLA_KV_LORA + MLA_ROPE]
    q = _dot(wqb_ref[...], _rms_cols(qa, gqa_ref[...]).astype(BF16))
    kv = _dot(wkvb_ref[...], _rms_cols(kva, gkva_ref[...]).astype(BF16))
    cp, sp = rope_ref[0:ROPE_HALF], rope_ref[ROPE_HALF:2 * ROPE_HALF]
    krot = _rope_half(kr, cp, sp)
    zpad = jnp.zeros((MLA_QPAD - MLA_NOPE - MLA_ROPE, tm), F32)
    qh, kh, vh = [], [], []
    qw = MLA_NOPE + MLA_ROPE
    kw = MLA_NOPE + MLA_V
    for hd in range(MLA_HEADS):
        qh += [q[hd * qw:hd * qw + MLA_NOPE], _rope_half(q[hd * qw + MLA_NOPE:(hd + 1) * qw], cp, sp), zpad]
        kh += [kv[hd * kw:hd * kw + MLA_NOPE], krot, zpad]
        vh += [kv[hd * kw + MLA_NOPE:(hd + 1) * kw]]
    mq_ref[...] = (jnp.concatenate(qh, axis=0) * (qw ** -0.5)).astype(BF16)
    mk_ref[...] = jnp.concatenate(kh, axis=0).T.astype(BF16)
    mv_ref[...] = jnp.concatenate(vh, axis=0).astype(BF16)


def proj_b(xt, g1, wb, gqa, gkva, wqb, wkvb, rope, tm=256):
    d, tg = xt.shape
    cspec = lambda f: pl.BlockSpec((f, tm), lambda i: (0, i))
    qf = MLA_HEADS * MLA_QPAD
    return pl.pallas_call(
        _proj_b_kernel, grid=(tg // tm,),
        in_specs=[cspec(d), _const_spec((d, LANES)), _const_spec(wb.shape),
                  _const_spec((MLA_Q_LORA, LANES)), _const_spec((MLA_KV_LORA, LANES)),
                  _const_spec(wqb.shape), _const_spec(wkvb.shape), cspec(6 * ROPE_HALF)],
        out_specs=[cspec(qf), pl.BlockSpec((tm, qf), lambda i: (i, 0)), cspec(MLA_HEADS * MLA_V)],
        out_shape=[jax.ShapeDtypeStruct((qf, tg), BF16), jax.ShapeDtypeStruct((tg, qf), BF16),
                   jax.ShapeDtypeStruct((MLA_HEADS * MLA_V, tg), BF16)],
        compiler_params=_params("parallel"), name="proj_b")(xt, g1, wb, gqa, gkva, wqb, wkvb, rope)


def _proj_c_kernel(x_ref, g1_ref, w_ref, z_ref, xbc_ref, dt_ref):
    xn = _rms_cols(x_ref[...], g1_ref[...]).astype(BF16)
    z_ref[...] = _dot(w_ref[0:SSM_INNER, :], xn)
    xbc_ref[...] = _dot(w_ref[SSM_INNER:SSM_INNER + SSM_CONV_CH, :], xn)
    dt_ref[...] = _dot(w_ref[SSM_INNER + SSM_CONV_CH:SSM_INNER + SSM_CONV_CH + 2 * SSM_HEADS, :], xn)


def proj_c(xt, g1, wc, tm=256):
    d, tg = xt.shape
    cspec = lambda f: pl.BlockSpec((f, tm), lambda i: (0, i))
    col = lambda f: jax.ShapeDtypeStruct((f, tg), F32)
    return pl.pallas_call(
        _proj_c_kernel, grid=(tg // tm,),
        in_specs=[cspec(d), _const_spec((d, LANES)), _const_spec(wc.shape)],
        out_specs=[cspec(SSM_INNER), cspec(SSM_CONV_CH), cspec(2 * SSM_HEADS)],
        out_shape=[col(SSM_INNER), col(SSM_CONV_CH), col(2 * SSM_HEADS)],
        compiler_params=_params("parallel"), name="proj_c")(xt, g1, wc)


NA_QROWS = 2
NA_KBLK = 5


def _na_kernel(q_ref, *rest):
    k_refs = rest[0:NA_KBLK]
    v_refs = rest[NA_KBLK:2 * NA_KBLK]
    nb_ref = rest[2 * NA_KBLK]
    o_ref = rest[2 * NA_KBLK + 1]
    blk = NA_QROWS * GRID_W
    for h in range(NA_HEADS):
        hs = slice(h * HEAD_DIM, (h + 1) * HEAD_DIM)
        q = q_ref[hs, :]
        s = jnp.concatenate([_dot(k_refs[i][:, hs], q) for i in range(NA_KBLK)], axis=0) + nb_ref[0, h]
        m = jnp.max(s, axis=0, keepdims=True)
        p = jnp.exp(s - m)
        l = jnp.sum(p, axis=0, keepdims=True)
        pb = p.astype(BF16)
        o = _dot(v_refs[0][hs, :], pb[0:blk])
        for i in range(1, NA_KBLK):
            o = o + _dot(v_refs[i][hs, :], pb[i * blk:(i + 1) * blk])
        o_ref[hs, :] = o / l


def na_bias_table(rpb, rows):
    blk_rows = NA_QROWS
    nblk = rows // blk_rows
    js = np.array([0, 1, 2, nblk - 2, nblk - 1])
    kk = np.arange(NA_KBLK * blk_rows * GRID_W)
    qq = np.arange(blk_rows * GRID_W)
    start = np.clip(js - 2, 0, nblk - NA_KBLK) * blk_rows
    kr = start[:, None] + (kk // GRID_W)[None, :]
    kc = kk % GRID_W
    r = js[:, None] * blk_rows + (qq // GRID_W)[None, :]
    qc = qq % GRID_W
    ws = np.clip(r - NA_ROWS // 2, 0, rows - NA_ROWS)
    win_lo = np.clip(qc - NA_COLS // 2, 0, GRID_W - NA_COLS)
    valid = ((kr[:, :, None] >= ws[:, None, :]) & (kr[:, :, None] < ws[:, None, :] + NA_ROWS)
             & (kc[None, :, None] >= win_lo[None, None, :]) & (kc[None, :, None] < win_lo[None, None, :] + NA_COLS))
    d_row = np.clip(kr[:, :, None] - r[:, None, :] + (NA_ROWS - 1), 0, 2 * NA_ROWS - 2)
    d_col = np.clip(kc[None, :, None] - qc[None, None, :] + (NA_COLS - 1), 0, 2 * NA_COLS - 2)
    d_col = np.broadcast_to(d_col, d_row.shape)
    bias = rpb[:, d_row, d_col]
    return jnp.where(valid[None], bias, NEG).transpose(1, 0, 2, 3).astype(F32)


def na_attention(qT, k, vT, nbias, b, s):
    tg = qT.shape[1]
    blk = NA_QROWS * GRID_W
    nblk = s // blk

    def kidx(i):
        return lambda bi, j: (bi * nblk + jnp.clip(j - 2, 0, nblk - NA_KBLK) + i, 0)

    def vidx(i):
        return lambda bi, j: (0, bi * nblk + jnp.clip(j - 2, 0, nblk - NA_KBLK) + i)

    def cfg(bi, j):
        c = jnp.where(j == 0, 0, jnp.where(j == 1, 1, jnp.where(j == nblk - 2, 3, jnp.where(j == nblk - 1, 4, 2))))
        return (c, 0, 0, 0)

    in_specs = ([pl.BlockSpec((NA_W, blk), lambda bi, j: (0, bi * nblk + j))]
                + [pl.BlockSpec((blk, NA_W), kidx(i)) for i in range(NA_KBLK)]
                + [pl.BlockSpec((NA_W, blk), vidx(i)) for i in range(NA_KBLK)]
                + [pl.BlockSpec((1, NA_HEADS, NA_KBLK * blk, blk), cfg)])
    return pl.pallas_call(
        _na_kernel, grid=(b, nblk), in_specs=in_specs,
        out_specs=pl.BlockSpec((NA_W, blk), lambda bi, j: (0, bi * nblk + j)),
        out_shape=jax.ShapeDtypeStruct((NA_W, tg), F32),
        compiler_params=_params("parallel", "parallel"), name="na_attention",
    )(qT, *([k] * NA_KBLK), *([vT] * NA_KBLK), nbias)


def _flash_kernel(q_ref, k_ref, v_ref, o_ref, m_ref, l_ref, acc_ref):
    ki = pl.program_id(3)

    @pl.when(ki == 0)
    def _():
        m_ref[...] = jnp.full(m_ref.shape, NEG, F32)
        l_ref[...] = jnp.zeros(l_ref.shape, F32)
        acc_ref[...] = jnp.zeros(acc_ref.shape, F32)

    s = _dot(k_ref[...], q_ref[...])
    m_prev = m_ref[...]
    m_new = jnp.maximum(m_prev, jnp.max(s, axis=0, keepdims=True))
    alpha = jnp.exp(m_prev - m_new)
    p = jnp.exp(s - m_new)
    l_ref[...] = alpha * l_ref[...] + jnp.sum(p, axis=0, keepdims=True)
    acc_ref[...] = alpha * acc_ref[...] + _dot(v_ref[...], p.astype(BF16))
    m_ref[...] = m_new

    @pl.when(ki == pl.num_programs(3) - 1)
    def _():
        o_ref[...] = acc_ref[...] / l_ref[...]


def flash_attention(qT, k, vT, b, s, heads, kv_heads, dq, dv, tq=512, tk=512):
    tg = qT.shape[1]
    g = heads // kv_heads
    nq, nk = s // tq, s // tk
    return pl.pallas_call(
        _flash_kernel, grid=(b, heads, nq, nk),
        in_specs=[pl.BlockSpec((dq, tq), lambda bi, h, qi, ki: (h, bi * nq + qi)),
                  pl.BlockSpec((tk, dq), lambda bi, h, qi, ki: (bi * nk + ki, h // g)),
                  pl.BlockSpec((dv, tk), lambda bi, h, qi, ki: (h // g, bi * nk + ki))],
        out_specs=pl.BlockSpec((dv, tq), lambda bi, h, qi, ki: (h, bi * nq + qi)),
        out_shape=jax.ShapeDtypeStruct((heads * dv, tg), F32),
        scratch_shapes=[pltpu.VMEM((1, tq), F32), pltpu.VMEM((1, tq), F32), pltpu.VMEM((dv, tq), F32)],
        compiler_params=_params("parallel", "parallel", "parallel", "arbitrary"), name="flash_attention",
    )(qT, k, vT)


def _conv_kernel(prev_ref, cur_ref, next_ref, w_ref, b_ref, o_ref):
    i = pl.program_id(1)
    tn = cur_ref.shape[1]
    halo = SSM_CONV // 2
    prev = jnp.where(i > 0, prev_ref[...], 0.0)
    nxt = jnp.where(i < pl.num_programs(1) - 1, next_ref[...], 0.0)
    ext = jnp.concatenate([prev, cur_ref[...], nxt], axis=1)
    width = ext.shape[1]
    acc = _lanes(b_ref[...], tn)
    for kk in range(SSM_CONV):
        sh = halo - kk
        xs = ext if sh == 0 else pltpu.roll(ext, sh % width, axis=1)
        acc = acc + _lanes(w_ref[kk], tn) * xs[:, LANES:LANES + tn]
    o_ref[...] = acc * (1.0 / (1.0 + jnp.exp(-acc)))


def ssm_conv(xbcT, w, bias, b, s, tn=512):
    c, tg = xbcT.shape
    nt = s // tn
    r = tn // LANES
    last = tg // LANES - 1
    return pl.pallas_call(
        _conv_kernel, grid=(b, nt),
        in_specs=[pl.BlockSpec((c, LANES), lambda bi, i: (0, jnp.maximum((bi * nt + i) * r - 1, 0))),
                  pl.BlockSpec((c, tn), lambda bi, i: (0, bi * nt + i)),
                  pl.BlockSpec((c, LANES), lambda bi, i: (0, jnp.minimum((bi * nt + i + 1) * r, last))),
                  _const_spec(w.shape), _const_spec(bias.shape)],
        out_specs=pl.BlockSpec((c, tn), lambda bi, i: (0, bi * nt + i)),
        out_shape=jax.ShapeDtypeStruct((c, tg), F32),
        compiler_params=_params("parallel", "parallel"), name="ssm_conv")(xbcT, xbcT, xbcT, w, bias)


def _split3_bf16(x):
    a = x.astype(BF16)
    r1 = x - a.astype(F32)
    b_ = r1.astype(BF16)
    c = (r1 - b_.astype(F32)).astype(BF16)
    return a, b_, c


def _ssd_direction(xbc, dtr, a_log, dt_bias, d_skip, state_ref, reverse):
    q = SSM_CHUNK
    xs = xbc[0:SSM_INNER]
    bt = xbc[SSM_INNER:SSM_INNER + SSM_BC]
    ct = xbc[SSM_INNER + SSM_BC:SSM_INNER + 2 * SSM_BC]
    z = dtr + dt_bias
    dt = jnp.maximum(z, 0.0) + jnp.log1p(jnp.exp(-jnp.abs(z)))
    a = dt * (-jnp.exp(a_log))
    ri = lax.broadcasted_iota(jnp.int32, (q, q), 0)
    ci = lax.broadcasted_iota(jnp.int32, (q, q), 1)
    tri = jnp.where((ri >= ci) if reverse else (ri <= ci), 1.0, 0.0).astype(BF16)
    a1, a2, a3 = _split3_bf16(a)
    acum = _dot(a1, tri) + _dot(a2, tri) + _dot(a3, tri)
    acum_t = acum.T
    edge = 0 if reverse else q - 1
    a_last = jnp.broadcast_to(acum[:, edge:edge + 1], acum.shape)
    dte = jnp.exp(a_last - acum)
    cdec = jnp.exp(a_last)
    eacum = jnp.exp(acum)
    keep = (ri >= ci) if reverse else (ri <= ci)
    ys = []
    for g in range(SSM_GROUPS):
        gs = slice(g * SSM_STATE, (g + 1) * SSM_STATE)
        b_g = bt[gs]
        c_g = ct[gs].astype(BF16)
        cb_t = _dot(b_g.T.astype(BF16), c_g)
        r = SSM_HEADS // SSM_GROUPS
        xw, xdts = [], []
        for hh in range(r):
            h = g * r + hh
            x_h = xs[h * SSM_HEAD_DIM:(h + 1) * SSM_HEAD_DIM]
            xdt = x_h * dt[h:h + 1]
            xdts.append(xdt)
            xw.append(xdt * dte[h:h + 1])
        xw = jnp.concatenate(xw, axis=0).astype(BF16)
        new_state = lax.dot_general(xw, b_g.astype(BF16), (((1,), (1,)), ((), ())),
                                    preferred_element_type=F32)
        rows = slice(g * r * SSM_HEAD_DIM, (g + 1) * r * SSM_HEAD_DIM)
        prev = state_ref[rows, :]
        y_off = _dot(prev.astype(BF16), c_g)
        dec_rows = []
        for hh in range(r):
            h = g * r + hh
            seg = acum[h:h + 1] - acum_t[:, h:h + 1]
            m_t = (cb_t * jnp.exp(jnp.where(keep, seg, NEG))).astype(BF16)
            y_d = _dot(xdts[hh].astype(BF16), m_t)
            hs = slice(hh * SSM_HEAD_DIM, (hh + 1) * SSM_HEAD_DIM)
            x_h = xs[h * SSM_HEAD_DIM:(h + 1) * SSM_HEAD_DIM]
            ys.append(y_d + y_off[hs] * eacum[h:h + 1] + x_h * d_skip[h:h + 1])
            dec_rows.append(jnp.broadcast_to(cdec[h:h + 1], (SSM_HEAD_DIM, q)))
        state_ref[rows, :] = prev * jnp.concatenate(dec_rows, axis=0) + new_state
    return jnp.concatenate(ys, axis=0)


def _ssd_kernel(xf_ref, dtf_ref, xb_ref, dtb_ref, alog_ref, dtb_p_ref, dsk_ref, yf_ref, yb_ref, sf_ref, sb_ref):
    c = pl.program_id(1)

    @pl.when(c == 0)
    def _():
        sf_ref[...] = jnp.zeros(sf_ref.shape, F32)
        sb_ref[...] = jnp.zeros(sb_ref.shape, F32)

    hh = SSM_HEADS
    yf_ref[...] = _ssd_direction(xf_ref[...], dtf_ref[0:hh], alog_ref[0:hh], dtb_p_ref[0:hh], dsk_ref[0:hh],
                                 sf_ref, False)
    yb_ref[...] = _ssd_direction(xb_ref[...], dtb_ref[hh:2 * hh], alog_ref[hh:2 * hh], dtb_p_ref[hh:2 * hh],
                                 dsk_ref[hh:2 * hh], sb_ref, True)


def ssd_scan(xbc_act, dtT, a_log, dt_bias, d_skip, b, s):
    c, tg = xbc_act.shape
    nc = s // SSM_CHUNK
    fwd = lambda bi, ci: (0, bi * nc + ci)
    bwd = lambda bi, ci: (0, bi * nc + nc - 1 - ci)
    out = jax.ShapeDtypeStruct((SSM_INNER, tg), F32)
    return pl.pallas_call(
        _ssd_kernel, grid=(b, nc),
        in_specs=[pl.BlockSpec((c, SSM_CHUNK), fwd), pl.BlockSpec((2 * SSM_HEADS, SSM_CHUNK), fwd),
                  pl.BlockSpec((c, SSM_CHUNK), bwd), pl.BlockSpec((2 * SSM_HEADS, SSM_CHUNK), bwd),
                  _const_spec(a_log.shape), _const_spec(dt_bias.shape), _const_spec(d_skip.shape)],
        out_specs=[pl.BlockSpec((SSM_INNER, SSM_CHUNK), fwd), pl.BlockSpec((SSM_INNER, SSM_CHUNK), bwd)],
        out_shape=[out, out],
        scratch_shapes=[pltpu.VMEM((SSM_INNER, SSM_STATE), F32), pltpu.VMEM((SSM_INNER, SSM_STATE), F32)],
        compiler_params=_params("parallel", "arbitrary"), name="ssd_scan",
    )(xbc_act, dtT, xbc_act, dtT, a_log, dt_bias, d_skip)


def _out_proj_kernel(na_ref, mla_ref, gqa_ref, yf_ref, yb_ref, z_ref, x_ref,
                     gna_ref, gmla_ref, ggqa_ref, gssm_ref, g2_ref, w_ref, xo_ref, xn_ref):
    z = z_ref[...]
    y = (yf_ref[...] + yb_ref[...]) * (z * (1.0 / (1.0 + jnp.exp(-z))))
    mixed = jnp.concatenate([
        _rms_cols(na_ref[...], gna_ref[...]), _rms_cols(mla_ref[...], gmla_ref[...]),
        _rms_cols(gqa_ref[...], ggqa_ref[...]), _rms_cols(y, gssm_ref[...])], axis=0).astype(BF16)
    xo = x_ref[...] + _dot(w_ref[...], mixed)
    xo_ref[...] = xo
    xn_ref[...] = _rms_cols(xo, g2_ref[...]).astype(BF16)


def out_proj(o_na, o_mla, o_gqa, y_f, y_b, zT, xt, gna, gmla, ggqa, gssm, g2, w_outT, tm=256):
    d, tg = xt.shape
    cspec = lambda f: pl.BlockSpec((f, tm), lambda i: (0, i))
    gspec = lambda f: _const_spec((f, LANES))
    return pl.pallas_call(
        _out_proj_kernel, grid=(tg // tm,),
        in_specs=[cspec(NA_W), cspec(MLA_HEADS * MLA_V), cspec(GQA_W), cspec(SSM_INNER), cspec(SSM_INNER),
                  cspec(SSM_INNER), cspec(d), gspec(NA_W), gspec(MLA_HEADS * MLA_V), gspec(GQA_W),
                  gspec(SSM_INNER), gspec(d), _const_spec(w_outT.shape)],
        out_specs=[cspec(d), cspec(d)],
        out_shape=[jax.ShapeDtypeStruct((d, tg), F32), jax.ShapeDtypeStruct((d, tg), BF16)],
        compiler_params=_params("parallel"), name="out_proj",
    )(o_na, o_mla, o_gqa, y_f, y_b, zT, xt, gna, gmla, ggqa, gssm, g2, w_outT)


PEER_NTOP = PEER_TOPK + 1
PEER_TROWS = 24


def _top_rows(sc, n):
    vals = []
    for _ in range(n):
        m = jnp.max(sc, axis=0, keepdims=True)
        vals.append(m)
        sc = jnp.where(sc == m, NEG, sc)
    return vals


def _stack_rows(rows, total, tm):
    rid = lax.broadcasted_iota(jnp.int32, (total, tm), 0)
    out = jnp.full((total, tm), NEG, F32)
    for k, r in enumerate(rows):
        out = jnp.where(rid == k, r, out)
    return out


def _peer_route_kernel(xn_ref, wq_ref, sk_ref, th_ref, w1_ref, s2_ref, e2_ref, q_sc, sc_sc, tv_sc):
    tm = xn_ref.shape[1]
    half = N_KEYS
    q_sc[...] = _dot(wq_ref[...], xn_ref[...]).astype(BF16)

    def score_body(hp, carry):
        sc = _dot(sk_ref[hp], q_sc[pl.ds(pl.multiple_of(hp * half, half), half), :])
        sc_sc[hp] = sc
        tv_sc[hp] = _stack_rows(_top_rows(sc, PEER_NTOP), PEER_TROWS, tm)
        return carry

    lax.fori_loop(0, 2 * PEER_HEADS, score_body, 0)

    def head_body(h, carry):
        rid8 = lax.broadcasted_iota(jnp.int32, (8, tm), 0)
        rid16 = lax.broadcasted_iota(jnp.int32, (16, tm), 0)
        ta = tv_sc[2 * h]
        tb = tv_sc[2 * h + 1]
        pieces = [ta + tb[0:1]]
        for j in range(1, 8):
            cnt = PEER_NTOP // (j + 1)
            pieces.append(jnp.where(rid8 < cnt, ta[0:8] + tb[j:j + 1], NEG))
        pieces.append(jnp.where(rid16 < PEER_NTOP - 8, tb[8:24] + ta[0:1], NEG))
        best = _top_rows(jnp.concatenate(pieces, axis=0), PEER_NTOP)
        top = best[0]
        zsum = jnp.zeros((1, tm), F32)
        for kk in range(PEER_TOPK):
            zsum = zsum + jnp.exp(best[kk] - top)
        tau = 0.5 * (best[PEER_TOPK - 1] + best[PEER_TOPK])
        s1 = sc_sc[2 * h]
        s2 = sc_sc[2 * h + 1]
        rows = pl.ds(pl.multiple_of(h * half, half), half)
        th_ref[rows, :] = tau - s1
        w1_ref[rows, :] = jnp.exp(s1 - ta[0:1]) / zsum
        s2_ref[rows, :] = s2
        e2_ref[rows, :] = jnp.exp(s2 - tb[0:1])
        return carry

    lax.fori_loop(0, PEER_HEADS, head_body, 0)


def peer_route(xnT, wqT, subk, tm=256):
    d, tg = xnT.shape
    f = PEER_HEADS * N_KEYS
    cspec = lambda rows: pl.BlockSpec((rows, tm), lambda i: (0, i))
    out = jax.ShapeDtypeStruct((f, tg), F32)
    return pl.pallas_call(
        _peer_route_kernel, grid=(tg // tm,),
        in_specs=[cspec(d), _const_spec(wqT.shape), _const_spec(subk.shape)],
        out_specs=[cspec(f)] * 4, out_shape=[out] * 4,
        scratch_shapes=[pltpu.VMEM((wqT.shape[0], tm), BF16),
                        pltpu.VMEM((2 * PEER_HEADS, N_KEYS, tm), F32),
                        pltpu.VMEM((2 * PEER_HEADS, PEER_TROWS, tm), F32)],
        compiler_params=_params("parallel"), name="peer_route")(xnT, wqT, subk)


def _gelu(x):
    return 0.5 * x * (1.0 + lax.erf(x * (2.0 ** -0.5)))


def _peer_dense_kernel(xn_ref, u_ref, vt_ref, th_ref, w1_ref, s2_ref, e2_ref, x_ref, o_ref, acc_ref):
    j = pl.program_id(1)
    te, tm = u_ref.shape[0], xn_ref.shape[1]
    groups = te // N_KEYS

    @pl.when(j == 0)
    def _():
        acc_ref[...] = jnp.zeros(acc_ref.shape, F32)

    hid = _dot(u_ref[...], xn_ref[...])
    parts = []
    for g in range(groups):
        i1 = j * groups + g
        comb = jnp.zeros((N_KEYS, tm), F32)
        for h in range(PEER_HEADS):
            hs = slice(h * N_KEYS, (h + 1) * N_KEYS)
            th = th_ref[pl.ds(h * N_KEYS + i1, 1), :]
            w1 = w1_ref[pl.ds(h * N_KEYS + i1, 1), :]
            comb = comb + jnp.where(s2_ref[hs, :] >= th, e2_ref[hs, :] * w1, 0.0)
        parts.append((_gelu(hid[g * N_KEYS:(g + 1) * N_KEYS]) * comb).astype(BF16))
    acc_ref[...] += _dot(vt_ref[...], jnp.concatenate(parts, axis=0))

    @pl.when(j == pl.num_programs(1) - 1)
    def _():
        o_ref[...] = x_ref[...] + acc_ref[...]


def peer_dense(xnT, u, vT, th, w1, s2, e2, xt, tm=512, te=512):
    d, tg = xt.shape
    ne = u.shape[0]
    f = PEER_HEADS * N_KEYS
    tok = lambda rows: pl.BlockSpec((rows, tm), lambda i, j: (0, i))
    return pl.pallas_call(
        _peer_dense_kernel, grid=(tg // tm, ne // te),
        in_specs=[tok(d), pl.BlockSpec((te, d), lambda i, j: (j, 0)), pl.BlockSpec((d, te), lambda i, j: (0, j)),
                  tok(f), tok(f), tok(f), tok(f), tok(d)],
        out_specs=tok(d), out_shape=jax.ShapeDtypeStruct((d, tg), F32),
        scratch_shapes=[pltpu.VMEM((d, tm), F32)],
        compiler_params=_params("parallel", "arbitrary"), name="peer_dense",
    )(xnT, u, vT, th, w1, s2, e2, xt)


def _rep(v):
    return jnp.broadcast_to(v.astype(F32)[..., None], v.shape + (LANES,))


def _rope_tables(b, s):
    pos = jnp.tile(jnp.arange(s), b)
    freqs = ROPE_THETA ** (-jnp.arange(ROPE_HALF, dtype=F32) / ROPE_HALF)

    def cs(p):
        ang = freqs[:, None] * p.astype(F32)[None, :]
        return [jnp.cos(ang), jnp.sin(ang)]

    return jnp.concatenate(cs(pos) + cs(pos // GRID_W) + cs(pos % GRID_W), axis=0)


def _prep_weights(norm1, w_in, mla_q_a_norm, mla_kv_a_norm, mla_w_q_b, mla_w_kv_b, gqa_q_norm, gqa_k_norm,
                  ssm_conv_w, ssm_conv_b, ssm_a_log, ssm_dt_bias, ssm_d, ssm_norm, out_norm_na, out_norm_mla,
                  out_norm_gqa, w_out, norm2, peer_w_query, peer_sub_keys, peer_u, peer_v, final_norm):
    sizes = (NA_W, NA_W, NA_W, MLA_Q_LORA, MLA_KV_LORA, MLA_ROPE, GQA_W, GQA_KV_W, GQA_KV_W,
             SSM_INNER, SSM_CONV_CH, 2 * SSM_HEADS)
    offs = np.concatenate([[0], np.cumsum(sizes)])
    seg = lambda i: w_in[:, :, offs[i]:offs[i + 1]]
    t = lambda w: jnp.swapaxes(w, -1, -2).astype(BF16)
    depth = w_in.shape[0]
    return dict(
        g1=_rep(norm1),
        wa=t(jnp.concatenate([seg(0), seg(1), seg(2), seg(6), seg(7), seg(8)], axis=-1)),
        wb=t(jnp.concatenate([seg(3), seg(4), seg(5)], axis=-1)),
        wc=t(jnp.concatenate([seg(9), seg(10), seg(11)], axis=-1)),
        gqa_q=_rep(gqa_q_norm), gqa_k=_rep(gqa_k_norm),
        mla_gq=_rep(mla_q_a_norm), mla_gkv=_rep(mla_kv_a_norm),
        wqb=t(mla_w_q_b), wkvb=t(mla_w_kv_b),
        conv_w=_rep(ssm_conv_w), conv_b=_rep(ssm_conv_b),
        a_log=_rep(ssm_a_log.reshape(depth, 2 * SSM_HEADS)),
        dt_bias=_rep(ssm_dt_bias.reshape(depth, 2 * SSM_HEADS)),
        d_skip=_rep(ssm_d.reshape(depth, 2 * SSM_HEADS)),
        g_ssm=_rep(ssm_norm), g_na=_rep(out_norm_na), g_mla=_rep(out_norm_mla), g_gqa=_rep(out_norm_gqa),
        w_out=t(w_out), g2=_rep(norm2), wq=t(peer_w_query),
        subk=peer_sub_keys.reshape(depth, 2 * PEER_HEADS, N_KEYS, -1).astype(BF16),
        u=peer_u.astype(BF16), vT=t(peer_v), g_final=_rep(final_norm))


def _trunk(x, w, na_rpb):
    b, s, d = x.shape
    depth = na_rpb.shape[0]
    rope = _rope_tables(b, s)
    xt = transpose_in(x.reshape(b * s, d))
    for l in range(depth):
        na_q, na_k, na_v, g_q, g_k, g_v = proj_a(xt, w["g1"][l], w["wa"][l], w["gqa_q"][l], w["gqa_k"][l], rope)
        m_q, m_k, m_v = proj_b(xt, w["g1"][l], w["wb"][l], w["mla_gq"][l], w["mla_gkv"][l],
                               w["wqb"][l], w["wkvb"][l], rope)
        s_z, s_xbc, s_dt = proj_c(xt, w["g1"][l], w["wc"][l])
        o_na = na_attention(na_q, na_k, na_v, na_bias_table(na_rpb[l], s // GRID_W), b, s)
        o_mla = flash_attention(m_q, m_k, m_v, b, s, MLA_HEADS, MLA_HEADS, MLA_QPAD, MLA_V)
        o_gqa = flash_attention(g_q, g_k, g_v, b, s, GQA_HEADS, GQA_KV_HEADS, HEAD_DIM, HEAD_DIM)
        xbc_act = ssm_conv(s_xbc, w["conv_w"][l], w["conv_b"][l], b, s)
        y_f, y_b = ssd_scan(xbc_act, s_dt, w["a_log"][l], w["dt_bias"][l], w["d_skip"][l], b, s)
        xt, xn2 = out_proj(o_na, o_mla, o_gqa, y_f, y_b, s_z, xt, w["g_na"][l], w["g_mla"][l], w["g_gqa"][l],
                           w["g_ssm"][l], w["g2"][l], w["w_out"][l])
        th, w1, s2, e2 = peer_route(xn2, w["wq"][l], w["subk"][l])
        xt = peer_dense(xn2, w["u"][l], w["vT"][l], th, w1, s2, e2, xt)
    return final_norm_out(xt, w["g_final"]).reshape(b, s, d)


def kernel(x_prompt, x_sample, norm1, w_in, na_rpb, mla_q_a_norm, mla_kv_a_norm, mla_w_q_b, mla_w_kv_b,
           gqa_q_norm, gqa_k_norm, ssm_conv_w, ssm_conv_b, ssm_a_log, ssm_dt_bias, ssm_d, ssm_norm,
           out_norm_na, out_norm_mla, out_norm_gqa, w_out, norm2, peer_w_query, peer_sub_keys, peer_u, peer_v,
           final_norm):
    w = _prep_weights(norm1, w_in, mla_q_a_norm, mla_kv_a_norm, mla_w_q_b, mla_w_kv_b, gqa_q_norm, gqa_k_norm,
                      ssm_conv_w, ssm_conv_b, ssm_a_log, ssm_dt_bias, ssm_d, ssm_norm, out_norm_na,
                      out_norm_mla, out_norm_gqa, w_out, norm2, peer_w_query, peer_sub_keys, peer_u, peer_v,
                      final_norm)
    return (_trunk(x_prompt, w, na_rpb), _trunk(x_sample, w, na_rpb))
```
